```python
import jax, jax.numpy as jnp
from jax import lax
import numpy as np

D_MODEL = 1024
BATCH = 8
SEQ = 2048
DEPTH = 2

N_HEADS = 16
N_KV_GROUPS = 2
HEADS_PER_GROUP = N_HEADS // N_KV_GROUPS
HEAD_DIM = 64
N_KV_PARTS = 6
N_BRANCH = 3
L_CMP = 32
STRIDE_CMP = 16
L_SLC = 64
N_SEL = 16
WINDOW = 512
Q_CHUNK = 64
CONV_WIDTH = 31
FFN_HIDDEN = -(-(8 * D_MODEL) // (3 * 256)) * 256
EPS = 1e-6
NEG = -1e30
FORCE = 1e9

kernel_name = "yoco_conformer_conv_nsa_hybrid"


def rmsnorm(x, g):
    xf = x.astype(jnp.float32)
    y = xf * lax.rsqrt(jnp.mean(xf * xf, axis=-1, keepdims=True) + EPS)
    return (y * g.astype(jnp.float32)).astype(x.dtype)


def layernorm(x, g, b):
    xf = x.astype(jnp.float32)
    mu = jnp.mean(xf, axis=-1, keepdims=True)
    var = jnp.mean(jnp.square(xf - mu), axis=-1, keepdims=True)
    y = (xf - mu) * lax.rsqrt(var + EPS)
    return (y * g.astype(jnp.float32) + b.astype(jnp.float32)).astype(x.dtype)


def swiglu_ffn(x, norm_g, w_in, w_out):
    gate, up = jnp.split(rmsnorm(x, norm_g) @ w_in, 2, axis=-1)
    return (jax.nn.silu(gate) * up) @ w_out


def conformer_conv(x, norm_g, pw1_w, pw1_b, dw_w, dw_b, ln_g, ln_b, pw2_w, pw2_b):
    a, b = jnp.split(rmsnorm(x, norm_g) @ pw1_w + pw1_b, 2, axis=-1)
    h = a * jax.nn.sigmoid(b)
    h = lax.conv_general_dilated(
        h, dw_w[:, None, :], window_strides=(1,),
        padding=[(CONV_WIDTH - 1, 0)],
        dimension_numbers=('NWC', 'WIO', 'NWC'),
        feature_group_count=h.shape[-1]) + dw_b
    h = jax.nn.silu(layernorm(h, ln_g, ln_b))
    return h @ pw2_w + pw2_b


def shared_kv(h, kv_norm, w_kv, cmp_pos_k, cmp_pos_v, phi_k_w1, phi_k_w2, phi_v_w1, phi_v_w2):
    B, S, _ = h.shape
    kv = (rmsnorm(h, kv_norm) @ w_kv).reshape(B, S, N_KV_PARTS, N_KV_GROUPS, HEAD_DIM)
    kv = kv.transpose(2, 0, 3, 1, 4)
    k_cmp_raw, v_cmp_raw, k_slc, v_slc, k_win, v_win = kv
    n_cmp = (S - L_CMP) // STRIDE_CMP + 1
    idx = np.arange(n_cmp)[:, None] * STRIDE_CMP + np.arange(L_CMP)[None, :]

    def compress(raw, pos, w1, w2):
        blk = raw[:, :, idx] + pos
        blk = blk.reshape(B, N_KV_GROUPS, n_cmp, L_CMP * HEAD_DIM)
        return jax.nn.silu(blk @ w1) @ w2

    k_cmp = compress(k_cmp_raw, cmp_pos_k, phi_k_w1, phi_k_w2)
    v_cmp = compress(v_cmp_raw, cmp_pos_v, phi_v_w1, phi_v_w2)
    n_slc = S // L_SLC
    k_slc_blk = k_slc.reshape(B, N_KV_GROUPS, n_slc, L_SLC, HEAD_DIM)
    v_slc_blk = v_slc.reshape(B, N_KV_GROUPS, n_slc, L_SLC, HEAD_DIM)
    pad = ((0, 0), (0, 0), (WINDOW, 0), (0, 0))
    k_win_pad = jnp.pad(k_win, pad)
    v_win_pad = jnp.pad(v_win, pad)
    return (k_cmp, v_cmp, k_slc_blk, v_slc_blk, k_win_pad, v_win_pad)


def nsa_attention(x, norm_g, w_in, w_out, kv):
    k_cmp, v_cmp, k_slc_blk, v_slc_blk, k_win_pad, v_win_pad = kv
    B, S, _ = x.shape
    G, Hg, dh = N_KV_GROUPS, HEADS_PER_GROUP, HEAD_DIM
    n_chunk = S // Q_CHUNK
    n_cmp = k_cmp.shape[2]
    n_slc = k_slc_blk.shape[2]
    k_sel = min(N_SEL, n_slc)
    scale = HEAD_DIM ** -0.5

    proj = rmsnorm(x, norm_g) @ w_in
    q = proj[..., :N_HEADS * dh]
    gates = jax.nn.sigmoid(proj[..., N_HEADS * dh:])
    q = q.reshape(B, n_chunk, Q_CHUNK, G, Hg, dh).transpose(1, 0, 3, 4, 2, 5)
    gates = gates.reshape(B, n_chunk, Q_CHUNK, N_BRANCH, G, Hg).transpose(1, 3, 0, 4, 5, 2)
    starts = jnp.arange(n_chunk, dtype=jnp.int32) * Q_CHUNK

    slopes = jnp.asarray(2.0 ** (-8.0 * (np.arange(N_HEADS) + 1) / N_HEADS), jnp.float32)
    slopes = slopes.reshape(G, Hg)[:, :, None, None]
    cmp_end = jnp.arange(n_cmp, dtype=jnp.int32) * STRIDE_CMP + (L_CMP - 1)
    cs = np.arange(n_cmp)[:, None] * STRIDE_CMP
    ss = np.arange(n_slc)[None, :] * L_SLC
    overlap = jnp.asarray((cs <= ss + L_SLC - 1) & (cs + L_CMP - 1 >= ss), jnp.float32)
    blk = jnp.arange(n_slc, dtype=jnp.int32)
    bidx = jnp.arange(B)[:, None, None, None]
    gidx = jnp.arange(G)[None, :, None, None]

    def chunk_fn(args):
        q_c, g_c, c0 = args
        t = c0 + jnp.arange(Q_CHUNK, dtype=jnp.int32)

        dist = (t[:, None] - cmp_end[None, :]).astype(jnp.float32)
        valid = dist >= 0
        s = jnp.einsum('bghqd,bgnd->bghqn', q_c, k_cmp).astype(jnp.float32) * scale
        s = jnp.where(valid, s - slopes * dist, NEG)
        p_cmp = jax.nn.softmax(s, axis=-1) * valid.any(-1, keepdims=True)
        o_cmp = jnp.einsum('bghqn,bgnd->bghqd', p_cmp.astype(v_cmp.dtype), v_cmp)

        imp = jnp.einsum('bghqn,nj->bgqj', p_cmp, overlap)
        t_blk = t // L_SLC
        forced = (blk[None] == 0) | (blk[None] == t_blk[:, None]) | (blk[None] == t_blk[:, None] - 1)
        causal = blk[None] * L_SLC <= t[:, None]
        imp = jnp.where(forced, FORCE, jnp.where(causal, imp, -FORCE))
        _, sel = lax.top_k(imp, k_sel)
        ks = k_slc_blk[bidx, gidx, sel]
        vs = v_slc_blk[bidx, gidx, sel]
        pos = sel[..., None] * L_SLC + jnp.arange(L_SLC, dtype=jnp.int32)
        dist = (t[None, None, :, None, None] - pos).astype(jnp.float32)[:, :, None]
        s = jnp.einsum('bghqd,bgqkld->bghqkl', q_c, ks).astype(jnp.float32) * scale
        s = jnp.where(dist >= 0, s - slopes[..., None] * dist, NEG)
        p = jax.nn.softmax(s.reshape(B, G, Hg, Q_CHUNK, k_sel * L_SLC), axis=-1)
        o_slc = jnp.einsum('bghqm,bgqmd->bghqd', p.astype(vs.dtype),
                           vs.reshape(B, G, Q_CHUNK, k_sel * L_SLC, dh))

        kw = lax.dynamic_slice_in_dim(k_win_pad, c0, WINDOW + Q_CHUNK, axis=2)
        vw = lax.dynamic_slice_in_dim(v_win_pad, c0, WINDOW + Q_CHUNK, axis=2)
        s_pos = c0 - WINDOW + jnp.arange(WINDOW + Q_CHUNK, dtype=jnp.int32)
        dist = (t[:, None] - s_pos[None, :]).astype(jnp.float32)
        ok = (dist >= 0) & (dist < WINDOW) & (s_pos[None, :] >= 0)
        s = jnp.einsum('bghqd,bgmd->bghqm', q_c, kw).astype(jnp.float32) * scale
        s = jnp.where(ok, s - slopes * dist, NEG)
        p = jax.nn.softmax(s, axis=-1)
        o_win = jnp.einsum('bghqm,bgmd->bghqd', p.astype(vw.dtype), vw)

        o = (g_c[0][..., None] * o_cmp + g_c[1][..., None] * o_slc
             + g_c[2][..., None] * o_win)
        return o.transpose(0, 3, 1, 2, 4).reshape(B, Q_CHUNK, N_HEADS * dh).astype(x.dtype)

    out = lax.map(chunk_fn, (q, gates, starts))
    out = out.transpose(1, 0, 2, 3).reshape(B, S, N_HEADS * dh)
    return out @ w_out


def setup_inputs(seed: int = 0) -> dict:
    key = jax.random.key(seed)
    ks = jax.random.split(key, 32)
    n_a = DEPTH // 2
    n_b = DEPTH - n_a
    D, F = D_MODEL, FFN_HIDDEN
    f32 = jnp.float32

    def w(k, shape, fan_in):
        return jax.random.normal(k, shape, f32) * fan_in ** -0.5

    def gain(k, shape):
        return 1.0 + 0.05 * jax.random.normal(k, shape, f32)

    def bias(k, shape):
        return 0.02 * jax.random.normal(k, shape, f32)

    return {
        "x": jax.random.normal(ks[0], (BATCH, SEQ, D), f32),
        "a_norm": gain(ks[1], (n_a, D)),
        "a_pw1_w": w(ks[2], (n_a, D, 2 * D), D),
        "a_pw1_b": bias(ks[3], (n_a, 2 * D)),
        "a_dw_w": w(ks[4], (n_a, CONV_WIDTH, D), CONV_WIDTH),
        "a_dw_b": bias(ks[5], (n_a, D)),
        "a_ln_g": gain(ks[6], (n_a, D)),
        "a_ln_b": bias(ks[7], (n_a, D)),
        "a_pw2_w": w(ks[8], (n_a, D, D), D),
        "a_pw2_b": bias(ks[9], (n_a, D)),
        "kv_norm": gain(ks[10], (D,)),
        "w_kv": w(ks[11], (D, N_KV_PARTS * N_KV_GROUPS * HEAD_DIM), D),
        "cmp_pos_k": 0.1 * jax.random.normal(ks[12], (L_CMP, HEAD_DIM), f32),
        "cmp_pos_v": 0.1 * jax.random.normal(ks[13], (L_CMP, HEAD_DIM), f32),
        "phi_k_w1": w(ks[14], (L_CMP * HEAD_DIM, HEAD_DIM), L_CMP * HEAD_DIM),
        "phi_k_w2": w(ks[15], (HEAD_DIM, HEAD_DIM), HEAD_DIM),
        "phi_v_w1": w(ks[16], (L_CMP * HEAD_DIM, HEAD_DIM), L_CMP * HEAD_DIM),
        "phi_v_w2": w(ks[17], (HEAD_DIM, HEAD_DIM), HEAD_DIM),
        "b_norm": gain(ks[18], (n_b, D)),
        "b_w_in": w(ks[19], (n_b, D, N_HEADS * HEAD_DIM + N_BRANCH * N_HEADS), D),
        "b_w_out": w(ks[20], (n_b, N_HEADS * HEAD_DIM, D), N_HEADS * HEAD_DIM),
        "ffn_norm": gain(ks[21], (DEPTH, D)),
        "ffn_w_in": w(ks[22], (DEPTH, D, 2 * F), D),
        "ffn_w_out": w(ks[23], (DEPTH, F, D), F),
        "final_norm": gain(ks[24], (D,)),
    }


def reference(x, a_norm, a_pw1_w, a_pw1_b, a_dw_w, a_dw_b, a_ln_g, a_ln_b, a_pw2_w, a_pw2_b,
              kv_norm, w_kv, cmp_pos_k, cmp_pos_v, phi_k_w1, phi_k_w2, phi_v_w1, phi_v_w2,
              b_norm, b_w_in, b_w_out, ffn_norm, ffn_w_in, ffn_w_out, final_norm):
    n_a = DEPTH // 2
    h = x
    kv = None
    for layer in range(DEPTH):
        if layer < n_a:
            i = layer
            h = h + conformer_conv(h, a_norm[i], a_pw1_w[i], a_pw1_b[i], a_dw_w[i], a_dw_b[i],
                                   a_ln_g[i], a_ln_b[i], a_pw2_w[i], a_pw2_b[i])
        else:
            j = layer - n_a
            h = h + nsa_attention(h, b_norm[j], b_w_in[j], b_w_out[j], kv)
        h = h + swiglu_ffn(h, ffn_norm[layer], ffn_w_in[layer], ffn_w_out[layer])
        if layer == n_a - 1:
            kv = shared_kv(h, kv_norm, w_kv, cmp_pos_k, cmp_pos_v,
                           phi_k_w1, phi_k_w2, phi_v_w1, phi_v_w2)
    return rmsnorm(h, final_norm)
```

```python
import functools

import numpy as np
import jax
import jax.numpy as jnp
from jax import lax
from jax.experimental import pallas as pl
from jax.experimental.pallas import tpu as pltpu

D_MODEL = 1024
BATCH = 8
SEQ = 2048
DEPTH = 2
N_HEADS = 16
N_KV_GROUPS = 2
HEADS_PER_GROUP = N_HEADS // N_KV_GROUPS
HEAD_DIM = 64
N_KV_PARTS = 6
N_BRANCH = 3
L_CMP = 32
STRIDE_CMP = 16
L_SLC = 64
N_SEL = 16
WINDOW = 512
CONV_WIDTH = 31
FFN_HIDDEN = -(-(8 * D_MODEL) // (3 * 256)) * 256
EPS = 1e-6
NEG = -1e30
FORCE = 1e9

TOKENS = BATCH * SEQ
N_CMP = (SEQ - L_CMP) // STRIDE_CMP + 1
N_SLC = SEQ // L_SLC
LANES = 128
PAIR = N_KV_GROUPS * HEAD_DIM
assert PAIR == LANES and N_CMP < LANES and N_SLC <= LANES

F32 = jnp.float32
BF16 = jnp.bfloat16

VMEM_LIMIT = 56 * 1024 * 1024


def _cparams(n_axes):
    return pltpu.CompilerParams(dimension_semantics=("arbitrary",) * n_axes,
                                vmem_limit_bytes=VMEM_LIMIT)


def _dot(a, b):
    return jnp.dot(a, b, preferred_element_type=F32)


def _dot_nt(a, b):
    return lax.dot_general(a, b, (((1,), (1,)), ((), ())), preferred_element_type=F32)


def _dot3(x, w):
    hi = x.astype(BF16)
    r1 = x - hi.astype(F32)
    mid = r1.astype(BF16)
    lo = (r1 - mid.astype(F32)).astype(BF16)
    return _dot(hi, w) + _dot(mid, w) + _dot(lo, w)


def _rms_unit(x):
    return x * lax.rsqrt(jnp.mean(x * x, axis=-1, keepdims=True) + EPS)


PW1_TM, PW1_TN = 1024, 512


def _pw1_glu_kernel(x_ref, g_ref, wa_ref, wb_ref, ba_ref, bb_ref, o_ref, xn_ref):
    @pl.when(pl.program_id(1) == 0)
    def _():
        xn_ref[...] = (_rms_unit(x_ref[...]) * g_ref[...]).astype(BF16)

    xn = xn_ref[...]
    a = _dot(xn, wa_ref[...]) + ba_ref[...]
    b = _dot(xn, wb_ref[...]) + bb_ref[...]
    o_ref[...] = a * jax.nn.sigmoid(b)


def _pw1_glu(x2, norm_g, w, b):
    d = D_MODEL
    nj = d // PW1_TN
    return pl.pallas_call(
        _pw1_glu_kernel,
        grid=(TOKENS // PW1_TM, nj),
        in_specs=[
            pl.BlockSpec((PW1_TM, d), lambda i, j: (i, 0)),
            pl.BlockSpec((1, d), lambda i, j: (0, 0)),
            pl.BlockSpec((d, PW1_TN), lambda i, j: (0, j)),
            pl.BlockSpec((d, PW1_TN), lambda i, j: (0, j + nj)),
            pl.BlockSpec((1, PW1_TN), lambda i, j: (0, j)),
            pl.BlockSpec((1, PW1_TN), lambda i, j: (0, j + nj)),
        ],
        out_specs=pl.BlockSpec((PW1_TM, PW1_TN), lambda i, j: (i, j)),
        out_shape=jax.ShapeDtypeStruct((TOKENS, d), F32),
        scratch_shapes=[pltpu.VMEM((PW1_TM, d), BF16)],
        compiler_params=_cparams(2),
        name="pw1_glu",
    )(x2, norm_g, w, w, b, b)


CONV_TS = 256
CONV_HALO = 32
CONV_RC = 64
assert CONV_HALO >= CONV_WIDTH - 1 and SEQ % CONV_TS == 0 and CONV_TS % CONV_RC == 0


def _conv_pw2_kernel(h_ref, x_ref, dww_ref, dwb_ref, lng_ref, lnb_ref, w2_ref, b2_ref, o_ref,
                     buf_ref, conv_ref):
    s = pl.program_id(1)

    @pl.when(s == 0)
    def _():
        buf_ref[0:CONV_HALO, :] = jnp.zeros((CONV_HALO, D_MODEL), F32)

    @pl.when(s > 0)
    def _():
        buf_ref[0:CONV_HALO, :] = buf_ref[CONV_TS:CONV_TS + CONV_HALO, :]

    buf_ref[CONV_HALO:CONV_HALO + CONV_TS, :] = h_ref[...]

    base = CONV_HALO - (CONV_WIDTH - 1)
    for c in range(D_MODEL // LANES):
        cs = slice(c * LANES, (c + 1) * LANES)
        w = dww_ref[:, cs]
        bias = dwb_ref[:, cs]
        for rc in range(CONV_TS // CONV_RC):
            r0 = rc * CONV_RC
            acc = jnp.broadcast_to(bias, (CONV_RC, LANES))
            for k in range(CONV_WIDTH):
                acc = acc + w[k:k + 1, :] * buf_ref[r0 + base + k:r0 + base + k + CONV_RC, cs]
            conv_ref[r0:r0 + CONV_RC, cs] = acc

    y = conv_ref[...]
    mu = jnp.mean(y, axis=-1, keepdims=True)
    yc = y - mu
    var = jnp.mean(yc * yc, axis=-1, keepdims=True)
    yn = yc * lax.rsqrt(var + EPS) * lng_ref[...] + lnb_ref[...]
    z = yn * jax.nn.sigmoid(yn)
    o_ref[...] = x_ref[...] + (_dot(z.astype(BF16), w2_ref[...]) + b2_ref[...])


def _conv_pw2(hglu, x2, dw_w, dw_b, ln_g, ln_b, w2, b2):
    d = D_MODEL
    ns = SEQ // CONV_TS
    row = lambda b, s: (b * ns + s, 0)
    const = lambda b, s: (0, 0)
    return pl.pallas_call(
        _conv_pw2_kernel,
        grid=(BATCH, ns),
        in_specs=[
            pl.BlockSpec((CONV_TS, d), row),
            pl.BlockSpec((CONV_TS, d), row),
            pl.BlockSpec((CONV_WIDTH, d), const),
            pl.BlockSpec((1, d), const),
            pl.BlockSpec((1, d), const),
            pl.BlockSpec((1, d), const),
            pl.BlockSpec((d, d), const),
            pl.BlockSpec((1, d), const),
        ],
        out_specs=pl.BlockSpec((CONV_TS, d), row),
        out_shape=jax.ShapeDtypeStruct((TOKENS, d), F32),
        scratch_shapes=[pltpu.VMEM((CONV_HALO + CONV_TS, d), F32), pltpu.VMEM((CONV_TS, d), F32)],
        compiler_params=_cparams(2),
        name="conv_pw2",
    )(hglu, x2, dw_w, dw_b, ln_g, ln_b, w2, b2)


FFN_TM = 512
FFN_FC = FFN_HIDDEN // 2
assert FFN_FC % LANES == 0


def _ffn_kernel(x_ref, g_ref, wg_ref, wu_ref, wo_ref, fg_ref, o_ref, xn_ref, acc_ref, *, n_f, final_norm):
    f = pl.program_id(1)

    @pl.when(f == 0)
    def _():
        xn_ref[...] = (_rms_unit(x_ref[...]) * g_ref[...]).astype(BF16)
        acc_ref[...] = jnp.zeros_like(acc_ref)

    xn = xn_ref[...]
    gate = _dot(xn, wg_ref[...])
    up = _dot(xn, wu_ref[...])
    hid = (gate * jax.nn.sigmoid(gate) * up).astype(BF16)
    acc_ref[...] += _dot(hid, wo_ref[...])

    @pl.when(f == n_f - 1)
    def _():
        y = x_ref[...] + acc_ref[...]
        if final_norm:
            y = _rms_unit(y) * fg_ref[...]
        o_ref[...] = y


def _ffn(h2, norm_g, w_in, w_out, final_g, final_norm):
    d = D_MODEL
    n_f = FFN_HIDDEN // FFN_FC
    return pl.pallas_call(
        functools.partial(_ffn_kernel, n_f=n_f, final_norm=final_norm),
        grid=(TOKENS // FFN_TM, n_f),
        in_specs=[
            pl.BlockSpec((FFN_TM, d), lambda i, f: (i, 0)),
            pl.BlockSpec((1, d), lambda i, f: (0, 0)),
            pl.BlockSpec((d, FFN_FC), lambda i, f: (0, f)),
            pl.BlockSpec((d, FFN_FC), lambda i, f: (0, f + n_f)),
            pl.BlockSpec((FFN_FC, d), lambda i, f: (f, 0)),
            pl.BlockSpec((1, d), lambda i, f: (0, 0)),
        ],
        out_specs=pl.BlockSpec((FFN_TM, d), lambda i, f: (i, 0)),
        out_shape=jax.ShapeDtypeStruct((TOKENS, d), F32),
        scratch_shapes=[pltpu.VMEM((FFN_TM, d), BF16), pltpu.VMEM((FFN_TM, d), F32)],
        compiler_params=_cparams(2),
        name="ffn",
    )(h2, norm_g, w_in, w_in, w_out, final_g)


PROJ_TM = 512
KV_W = N_KV_PARTS * PAIR
KVC_W = 2 * PAIR
KVS_W = KV_W - KVC_W
Q_W = N_HEADS * HEAD_DIM
N_GATE = N_BRANCH * N_HEADS


def _kvq_proj_kernel(h_ref, gkv_ref, gq_ref, wkv_ref, wq_ref, wgate_ref, kvc_ref, kvs_ref, q_ref, gate_ref):
    r = _rms_unit(h_ref[...])
    xkv = (r * gkv_ref[...]).astype(BF16)
    xq = (r * gq_ref[...]).astype(BF16)
    kv = _dot(xkv, wkv_ref[...])
    kvc_ref[...] = kv[:, :KVC_W]
    kvs_ref[...] = kv[:, KVC_W:].astype(BF16)
    q = _dot(xq, wq_ref[...]) * (HEAD_DIM ** -0.5)
    for j in range(HEADS_PER_GROUP):
        q_ref[j] = q[:, j * PAIR:(j + 1) * PAIR].astype(BF16)
    gate_ref[...] = jax.nn.sigmoid(_dot(xq, wgate_ref[...]))


def _kvq_proj(h2, kv_norm, q_norm, wkv, wq, wgate):
    d = D_MODEL
    const = lambda i: (0, 0)
    return pl.pallas_call(
        _kvq_proj_kernel,
        grid=(TOKENS // PROJ_TM,),
        in_specs=[
            pl.BlockSpec((PROJ_TM, d), lambda i: (i, 0)),
            pl.BlockSpec((1, d), const),
            pl.BlockSpec((1, d), const),
            pl.BlockSpec((d, KV_W), const),
            pl.BlockSpec((d, Q_W), const),
            pl.BlockSpec((d, LANES), const),
        ],
        out_specs=[
            pl.BlockSpec((PROJ_TM, KVC_W), lambda i: (i, 0)),
            pl.BlockSpec((PROJ_TM, KVS_W), lambda i: (i, 0)),
            pl.BlockSpec((HEADS_PER_GROUP, PROJ_TM, PAIR), lambda i: (0, i, 0)),
            pl.BlockSpec((PROJ_TM, LANES), lambda i: (i, 0)),
        ],
        out_shape=[
            jax.ShapeDtypeStruct((TOKENS, KVC_W), F32),
            jax.ShapeDtypeStruct((TOKENS, KVS_W), BF16),
            jax.ShapeDtypeStruct((HEADS_PER_GROUP, TOKENS, PAIR), BF16),
            jax.ShapeDtypeStruct((TOKENS, LANES), F32),
        ],
        compiler_params=_cparams(1),
        name="kvq_proj",
    )(h2, kv_norm, q_norm, wkv, wq, wgate)


CMP_ROWS = SEQ // STRIDE_CMP
CMP_HALVES = L_CMP // STRIDE_CMP
assert CMP_HALVES == 2 and CMP_ROWS == LANES


def _compress_kernel(x_ref, w1k_ref, w1v_ref, w2k_ref, w2v_ref, pk_ref, pv_ref, ko_ref, vo_ref, sh_ref):
    keep = lax.broadcasted_iota(jnp.int32, (CMP_ROWS, LANES), 0) < N_CMP
    parts = ((w1k_ref, w2k_ref, pk_ref, ko_ref), (w1v_ref, w2v_ref, pv_ref, vo_ref))
    for p, (w1_ref, w2_ref, pos_ref, out_ref) in enumerate(parts):
        first = jnp.zeros((CMP_ROWS, LANES), F32)
        second = jnp.zeros((CMP_ROWS, LANES), F32)
        for l in range(STRIDE_CMP):
            col = (2 * l + p) * LANES
            xt = x_ref[:, col:col + LANES]
            first = first + _dot((xt + pos_ref[l:l + 1, :]).astype(BF16), w1_ref[l])
            l2 = STRIDE_CMP + l
            second = second + _dot((xt + pos_ref[l2:l2 + 1, :]).astype(BF16), w1_ref[l2])
        sh_ref[0:CMP_ROWS, :] = second
        sh_ref[CMP_ROWS:CMP_ROWS + 8, :] = jnp.zeros((8, LANES), F32)
        hid = first + sh_ref[1:CMP_ROWS + 1, :]
        hid = hid * jax.nn.sigmoid(hid)
        res = _dot(hid.astype(BF16), w2_ref[...])
        out_ref[...] = jnp.where(keep, res, 0.0).astype(BF16)


def _compress(kvc, w1k, w1v, w2k, w2v, pos_k, pos_v):
    xw = STRIDE_CMP * KVC_W
    x3 = kvc.reshape(BATCH, CMP_ROWS, xw)
    c3 = lambda b: (0, 0, 0)
    c2 = lambda b: (0, 0)
    out_spec = pl.BlockSpec((None, CMP_ROWS, LANES), lambda b: (b, 0, 0))
    out_shape = jax.ShapeDtypeStruct((BATCH, CMP_ROWS, LANES), BF16)
    return pl.pallas_call(
        _compress_kernel,
        grid=(BATCH,),
        in_specs=[
            pl.BlockSpec((None, CMP_ROWS, xw), lambda b: (b, 0, 0)),
            pl.BlockSpec((L_CMP, LANES, LANES), c3),
            pl.BlockSpec((L_CMP, LANES, LANES), c3),
            pl.BlockSpec((LANES, LANES), c2),
            pl.BlockSpec((LANES, LANES), c2),
            pl.BlockSpec((L_CMP, LANES), c2),
            pl.BlockSpec((L_CMP, LANES), c2),
        ],
        out_specs=[out_spec, out_spec],
        out_shape=[out_shape, out_shape],
        scratch_shapes=[pltpu.VMEM((CMP_ROWS + 8, LANES), F32)],
        compiler_params=_cparams(1),
        name="compress",
    )(x3, w1k, w1v, w2k, w2v, pos_k, pos_v)


NSA_TQ = 128
NSA_KC = 512
NSA_WK = WINDOW + NSA_TQ
N_KC = SEQ // NSA_KC
assert SEQ % NSA_KC == 0 and NSA_KC % L_SLC == 0 and WINDOW % NSA_TQ == 0


def _softmax_parts(s):
    m = jnp.max(s, axis=-1, keepdims=True)
    e = jnp.exp(s - m)
    return e, jnp.sum(e, axis=-1, keepdims=True)


def _nsa_kernel(slopes_ref, q_ref, gate_ref, kcmp_ref, vcmp_ref, kslc_ref, vslc_ref, kwin_ref, vwin_ref,
                ovl_ref, e2_ref, ge_ref, o_ref, oacc_ref, psum_ref, msel_ref, gx_ref, mwin_ref):
    tq = NSA_TQ
    c0 = pl.program_id(1) * tq
    row = lax.broadcasted_iota(jnp.int32, (tq, LANES), 0)
    lane = lax.broadcasted_iota(jnp.int32, (tq, LANES), 1)
    t = c0 + row
    lo_half = lane < HEAD_DIM
    halves = (lo_half, jnp.logical_not(lo_half))

    gx = _dot3(gate_ref[...], ge_ref[...])
    for i in range(N_BRANCH * HEADS_PER_GROUP):
        gx_ref[i] = gx[:, i * LANES:(i + 1) * LANES]

    cmp_end = lane * STRIDE_CMP + (L_CMP - 1)
    valid_c = jnp.logical_and(t >= cmp_end, lane < N_CMP)
    mask_c = jnp.where(valid_c, 0.0, NEG)
    any_c = (t >= L_CMP - 1).astype(F32)
    cend_row = cmp_end[0:1, :].astype(F32)
    psum_ref[...] = jnp.zeros_like(psum_ref)
    kc_all = kcmp_ref[...]
    vc_all = vcmp_ref[...]

    def cmp_body(j, carry):
        qj = q_ref[j].astype(F32)
        outs = []
        for g in range(N_KV_GROUPS):
            qm = jnp.where(halves[g], qj, 0.0).astype(BF16)
            slope = slopes_ref[g * HEADS_PER_GROUP + j]
            s = _dot_nt(qm, kc_all) + slope * cend_row + mask_c
            e, l = _softmax_parts(s)
            p = e * (any_c / l)
            psum_ref[g] += p
            outs.append(_dot(p.astype(BF16), vc_all))
        oacc_ref[j] = gx_ref[N_BRANCH * j] * jnp.where(lo_half, outs[0], outs[1])
        return carry

    lax.fori_loop(0, HEADS_PER_GROUP, cmp_body, 0)

    t_blk = jnp.right_shift(t, 6)
    forced = jnp.logical_or(lane == 0, jnp.logical_or(lane == t_blk, lane == t_blk - 1))
    causal_blk = lane * L_SLC <= t
    n_kc = jnp.right_shift(c0 + tq + NSA_KC - 1, 9)
    lane_kc = lax.broadcasted_iota(jnp.int32, (tq, NSA_KC), 1)
    t_kc = c0 + lax.broadcasted_iota(jnp.int32, (tq, NSA_KC), 0)
    for g in range(N_KV_GROUPS):
        imp = _dot3(psum_ref[g], ovl_ref[...])
        imp = jnp.where(forced, FORCE, jnp.where(causal_blk, imp, -FORCE))
        rank = jnp.zeros((tq, LANES), F32)
        for jp in range(N_SLC):
            col = imp[:, jp:jp + 1]
            beats = jnp.logical_or(col > imp, jnp.logical_and(col == imp, lane > jp))
            rank = rank + beats.astype(F32)
        sel = jnp.logical_and(rank < N_SEL, lane < N_SLC).astype(BF16)

        def mask_body(kc, carry, sel=sel, g=g):
            selx = _dot(sel, e2_ref[kc])
            ok = jnp.logical_and(selx > 0.5, kc * NSA_KC + lane_kc <= t_kc)
            msel_ref[g, kc] = jnp.where(ok, 0.0, NEG)
            return carry

        lax.fori_loop(0, n_kc, mask_body, 0)

    ks = pl.multiple_of(jnp.maximum(c0 - WINDOW, 0), NSA_TQ)
    lane_w = lax.broadcasted_iota(jnp.int32, (tq, NSA_WK), 1)
    dist_w = (c0 - ks) + lax.broadcasted_iota(jnp.int32, (tq, NSA_WK), 0) - lane_w
    mwin_ref[...] = jnp.where(jnp.logical_and(dist_w >= 0, dist_w < WINDOW), 0.0, NEG)
    pos_w = (ks + lane_w[0:1, :]).astype(F32)
    pos_kc = lane_kc[0:1, :].astype(F32)

    def head_body(j, carry):
        qj = q_ref[j].astype(F32)
        o_slc = []
        o_win = []
        for g in range(N_KV_GROUPS):
            qm = jnp.where(halves[g], qj, 0.0).astype(BF16)
            slope = slopes_ref[g * HEADS_PER_GROUP + j]

            def kc_body(kc, mla, qm=qm, slope=slope, g=g):
                m, l, acc = mla
                k0 = pl.multiple_of(kc * NSA_KC, NSA_KC)
                kk = kslc_ref[pl.ds(k0, NSA_KC), :]
                vv = vslc_ref[pl.ds(k0, NSA_KC), :]
                bias = slope * (pos_kc + (kc * NSA_KC).astype(F32))
                s = _dot_nt(qm, kk) + bias + msel_ref[g, kc]
                m_new = jnp.maximum(m, jnp.max(s, axis=-1, keepdims=True))
                alpha = jnp.exp(m - m_new)
                e = jnp.exp(s - m_new)
                l = alpha * l + jnp.sum(e, axis=-1, keepdims=True)
                acc = alpha * acc + _dot(e.astype(BF16), vv)
                return m_new, l, acc

            init = (jnp.full((tq, 1), -jnp.inf, F32), jnp.zeros((tq, 1), F32), jnp.zeros((tq, LANES), F32))
            _, l, acc = lax.fori_loop(0, n_kc, kc_body, init)
            o_slc.append(acc / l)

            kk = kwin_ref[pl.ds(ks, NSA_WK), :]
            vv = vwin_ref[pl.ds(ks, NSA_WK), :]
            s = _dot_nt(qm, kk) + slope * pos_w + mwin_ref[...]
            e, l = _softmax_parts(s)
            o_win.append(_dot(e.astype(BF16), vv) / l)
        o = (oacc_ref[j]
             + gx_ref[N_BRANCH * j + 1] * jnp.where(lo_half, o_slc[0], o_slc[1])
             + gx_ref[N_BRANCH * j + 2] * jnp.where(lo_half, o_win[0], o_win[1]))
        o_ref[j] = o.astype(BF16)
        return carry

    lax.fori_loop(0, HEADS_PER_GROUP, head_body, 0)


def _nsa_constants():
    n = np.arange(LANES)[:, None]
    j = np.arange(LANES)[None, :]
    cs, ss = n * STRIDE_CMP, j * L_SLC
    ovl = (cs <= ss + L_SLC - 1) & (cs + L_CMP - 1 >= ss) & (n < N_CMP) & (j < N_SLC)
    key = np.arange(SEQ)[None, :]
    e2 = (key // L_SLC == np.arange(LANES)[:, None])
    e2 = e2.reshape(LANES, N_KC, NSA_KC).transpose(1, 0, 2)
    col = np.arange(N_BRANCH * HEADS_PER_GROUP * LANES)
    tile, ln = col // LANES, col % LANES
    jj, rr = tile // N_BRANCH, tile % N_BRANCH
    src = rr * N_HEADS + (ln // HEAD_DIM) * HEADS_PER_GROUP + jj
    ge = np.arange(LANES)[:, None] == src[None, :]
    slopes = (2.0 ** (-8.0 * (np.arange(N_HEADS) + 1) / N_HEADS)).astype(np.float32)
    as_bf16 = lambda a: jnp.asarray(a.astype(np.float32), BF16)
    return jnp.asarray(slopes), as_bf16(ovl), as_bf16(e2), as_bf16(ge)


def _nsa(q, gates, kcmp, vcmp, kvs):
    tq = NSA_TQ
    nq = SEQ // tq
    slopes, ovl, e2, ge = _nsa_constants()
    qo_spec = pl.BlockSpec((HEADS_PER_GROUP, tq, PAIR), lambda b, i: (0, b * nq + i, 0))
    cmp_spec = pl.BlockSpec((None, CMP_ROWS, LANES), lambda b, i: (b, 0, 0))
    kv_spec = lambda p: pl.BlockSpec((SEQ, PAIR), lambda b, i: (b, p))
    n_gx = N_BRANCH * HEADS_PER_GROUP
    return pl.pallas_call(
        _nsa_kernel,
        grid=(BATCH, nq),
        in_specs=[
            pl.BlockSpec(memory_space=pltpu.SMEM),
            qo_spec,
            pl.BlockSpec((tq, LANES), lambda b, i: (b * nq + i, 0)),
            cmp_spec, cmp_spec,
            kv_spec(0), kv_spec(1), kv_spec(2), kv_spec(3),
            pl.BlockSpec((LANES, LANES), lambda b, i: (0, 0)),
            pl.BlockSpec((N_KC, LANES, NSA_KC), lambda b, i: (0, 0, 0)),
            pl.BlockSpec((LANES, n_gx * LANES), lambda b, i: (0, 0)),
        ],
        out_specs=qo_spec,
        out_shape=jax.ShapeDtypeStruct((HEADS_PER_GROUP, TOKENS, PAIR), BF16),
        scratch_shapes=[
            pltpu.VMEM((HEADS_PER_GROUP, tq, LANES), F32),
            pltpu.VMEM((N_KV_GROUPS, tq, LANES), F32),
            pltpu.VMEM((N_KV_GROUPS, N_KC, tq, NSA_KC), F32),
            pltpu.VMEM((n_gx, tq, LANES), F32),
            pltpu.VMEM((tq, NSA_WK), F32),
        ],
        compiler_params=_cparams(2),
        name="nsa",
    )(slopes, q, gates, kcmp, vcmp, kvs, kvs, kvs, kvs, ovl, e2, ge)


OPROJ_TM = 512


def _oproj_kernel(o_ref, w_ref, h_ref, out_ref):
    acc = h_ref[...]
    for j in range(HEADS_PER_GROUP):
        acc = acc + _dot(o_ref[j], w_ref[j])
    out_ref[...] = acc


def _oproj(o, w, h2):
    d = D_MODEL
    return pl.pallas_call(
        _oproj_kernel,
        grid=(TOKENS // OPROJ_TM,),
        in_specs=[
            pl.BlockSpec((HEADS_PER_GROUP, OPROJ_TM, PAIR), lambda i: (0, i, 0)),
            pl.BlockSpec((HEADS_PER_GROUP, PAIR, d), lambda i: (0, 0, 0)),
            pl.BlockSpec((OPROJ_TM, d), lambda i: (i, 0)),
        ],
        out_specs=pl.BlockSpec((OPROJ_TM, d), lambda i: (i, 0)),
        out_shape=jax.ShapeDtypeStruct((TOKENS, d), F32),
        compiler_params=_cparams(1),
        name="oproj",
    )(o, w, h2)


def _block_diag2(w):
    z = jnp.zeros_like(w)
    top = jnp.concatenate([w, z], axis=-1)
    bot = jnp.concatenate([z, w], axis=-1)
    return jnp.concatenate([top, bot], axis=-2)


def kernel(x, a_norm, a_pw1_w, a_pw1_b, a_dw_w, a_dw_b, a_ln_g, a_ln_b, a_pw2_w, a_pw2_b,
           kv_norm, w_kv, cmp_pos_k, cmp_pos_v, phi_k_w1, phi_k_w2, phi_v_w1, phi_v_w2,
           b_norm, b_w_in, b_w_out, ffn_norm, ffn_w_in, ffn_w_out, final_norm):
    assert DEPTH == 2 and x.shape == (BATCH, SEQ, D_MODEL)
    d = D_MODEL
    row = lambda v: v.reshape(1, -1).astype(F32)
    h = x.reshape(TOKENS, d).astype(F32)

    hglu = _pw1_glu(h, row(a_norm[0]), a_pw1_w[0].astype(BF16), row(a_pw1_b[0]))
    h = _conv_pw2(hglu, h, a_dw_w[0].astype(F32), row(a_dw_b[0]), row(a_ln_g[0]), row(a_ln_b[0]),
                  a_pw2_w[0].astype(BF16), row(a_pw2_b[0]))
    h = _ffn(h, row(ffn_norm[0]), ffn_w_in[0].astype(BF16), ffn_w_out[0].astype(BF16),
             row(final_norm), final_norm=False)

    w_in = b_w_in[0]
    wq = w_in[:, :Q_W].reshape(d, N_KV_GROUPS, HEADS_PER_GROUP, HEAD_DIM).transpose(0, 2, 1, 3).reshape(d, Q_W)
    wgate = jnp.pad(w_in[:, Q_W:], ((0, 0), (0, LANES - N_GATE)))
    kvc, kvs, q, gates = _kvq_proj(h, row(kv_norm), row(b_norm[0]), w_kv.astype(BF16),
                                   wq.astype(BF16), wgate.astype(BF16))

    w1k = _block_diag2(phi_k_w1.reshape(L_CMP, HEAD_DIM, HEAD_DIM)).astype(BF16)
    w1v = _block_diag2(phi_v_w1.reshape(L_CMP, HEAD_DIM, HEAD_DIM)).astype(BF16)
    pos_k = jnp.tile(cmp_pos_k.astype(F32), (1, N_KV_GROUPS))
    pos_v = jnp.tile(cmp_pos_v.astype(F32), (1, N_KV_GROUPS))
    kcmp, vcmp = _compress(kvc, w1k, w1v, _block_diag2(phi_k_w2).astype(BF16),
                           _block_diag2(phi_v_w2).astype(BF16), pos_k, pos_v)

    o = _nsa(q, gates, kcmp, vcmp, kvs)
    w_o = b_w_out[0].reshape(N_KV_GROUPS, HEADS_PER_GROUP, HEAD_DIM, d).transpose(1, 0, 2, 3)
    h = _oproj(o, w_o.reshape(HEADS_PER_GROUP, PAIR, d).astype(BF16), h)
    h = _ffn(h, row(ffn_norm[1]), ffn_w_in[1].astype(BF16), ffn_w_out[1].astype(BF16),
             row(final_norm), final_norm=True)
    return h.reshape(BATCH, SEQ, d).astype(x.dtype)
```
